```python
import jax, jax.numpy as jnp
from jax import lax
import numpy as np

D_MODEL = 1024
BATCH = 16
SEQ = 256
DEPTH = 4
DEC_BATCH = 4
DEC_SEQ = 2048
PAST_LEN = 256

GRID_W = 64
N_EVEN = (DEPTH + 1) // 2
N_ODD = DEPTH // 2
FNET_WIDTH = D_MODEL // 2
FNET_GROUPS = 4
FNET_GROUP_DIM = FNET_WIDTH // FNET_GROUPS
GLA_HEADS = 4
GLA_DK = D_MODEL // 4
GLA_DV = D_MODEL // 2
GLA_DK_HEAD = GLA_DK // GLA_HEADS
GLA_DV_HEAD = GLA_DV // GLA_HEADS
GLA_GATE_RANK = 16
GLA_TAU = 16.0
GLA_CHUNK = 64
EVEN_SPLITS = (FNET_WIDTH, FNET_WIDTH + GLA_DK, FNET_WIDTH + 2 * GLA_DK, FNET_WIDTH + 2 * GLA_DK + GLA_DV, FNET_WIDTH + 2 * GLA_DK + 2 * GLA_DV)
EVEN_IN = FNET_WIDTH + 2 * GLA_DK + 2 * GLA_DV + 2 * GLA_GATE_RANK
ATT_HEADS = 8
ATT_KV_HEADS = 2
HEAD_DIM = D_MODEL // ATT_HEADS
ROPE_THETA = 10000.0
ROPE_PAIRS_PER_AXIS = HEAD_DIM // 4
Q_BLOCK = 128
ODD_SPLITS = (ATT_HEADS * HEAD_DIM, (ATT_HEADS + ATT_KV_HEADS) * HEAD_DIM)
ODD_IN = (ATT_HEADS + 2 * ATT_KV_HEADS) * HEAD_DIM
N_EXPERTS = 32
TOP_K = 4
D_FF = D_MODEL
SWIGLU_LIMIT = 7.0
SWIGLU_ALPHA = 1.702
LN_EPS = 1e-5
RMS_EPS = 1e-6
DEEPNORM_ALPHA = (2.0 * DEPTH) ** 0.25
DEEPNORM_BETA = (8.0 * DEPTH) ** -0.25

kernel_name = 'hybrid_fnet_gla_gqa_moe_diffusion_step'


def layer_norm(x, g=None, b=None):
    xf = x.astype(jnp.float32)
    mu = jnp.mean(xf, axis=-1, keepdims=True)
    var = jnp.mean(jnp.square(xf - mu), axis=-1, keepdims=True)
    y = (xf - mu) * lax.rsqrt(var + LN_EPS)
    if g is not None:
        y = y * g.astype(jnp.float32) + b.astype(jnp.float32)
    return y.astype(x.dtype)


def rms_norm(x, g):
    xf = x.astype(jnp.float32)
    y = xf * lax.rsqrt(jnp.mean(jnp.square(xf), axis=-1, keepdims=True) + RMS_EPS)
    return (y * g.astype(jnp.float32)).astype(x.dtype)


def modulation(cmod, w_mod_l, b_mod_l):
    m = jax.nn.silu(cmod) @ w_mod_l + b_mod_l
    return jnp.split(m[:, None, :], 6, axis=-1)


def axial_rope(n_tok):
    n_rows = n_tok // GRID_W
    row = jnp.broadcast_to(jnp.arange(n_rows, dtype=jnp.float32)[:, None], (n_rows, GRID_W)).reshape(-1)
    col = jnp.broadcast_to(jnp.arange(GRID_W, dtype=jnp.float32)[None, :], (n_rows, GRID_W)).reshape(-1)
    freqs = ROPE_THETA ** (-jnp.arange(ROPE_PAIRS_PER_AXIS, dtype=jnp.float32) / ROPE_PAIRS_PER_AXIS)
    ang = jnp.concatenate([row[:, None] * freqs, col[:, None] * freqs], axis=-1)
    return jnp.cos(ang), jnp.sin(ang)


def apply_rope(x, cos, sin):
    b, t, h, d = x.shape
    xp = x.astype(jnp.float32).reshape(b, t, h, d // 2, 2)
    x0, x1 = xp[..., 0], xp[..., 1]
    c = cos[None, :, None, :]
    s = sin[None, :, None, :]
    out = jnp.stack([x0 * c - x1 * s, x0 * s + x1 * c], axis=-1).reshape(b, t, h, d)
    return out.astype(x.dtype)


def block_attention(q, k, v):
    b, tq, h, hd = q.shape
    g = h // ATT_KV_HEADS
    nb = tq // Q_BLOCK
    qb = q.reshape(b, nb, Q_BLOCK, ATT_KV_HEADS, g, hd).transpose(1, 0, 2, 3, 4, 5)
    scale = hd ** -0.5

    def one_block(qblk):
        s = jnp.einsum('bqkgd,bskd->bkgqs', qblk, k, preferred_element_type=jnp.float32) * scale
        p = jax.nn.softmax(s, axis=-1)
        return jnp.einsum('bkgqs,bskd->bqkgd', p.astype(v.dtype), v)

    o = lax.map(one_block, qb)
    return o.transpose(1, 0, 2, 3, 4, 5).reshape(b, tq, h, hd)


def gla_chunk_scan(q, k, v, logg, s0):
    b, t, h, _ = q.shape
    dv = v.shape[-1]
    n = t // GLA_CHUNK

    def to_chunks(a):
        return a.reshape(b, n, GLA_CHUNK, h, a.shape[-1]).transpose(1, 0, 3, 2, 4)

    causal = jnp.tril(jnp.ones((GLA_CHUNK, GLA_CHUNK), dtype=bool))[:, :, None]

    def step(s, inp):
        qc, kc, vc, gc = inp
        cum = jnp.cumsum(gc, axis=-2)
        diff = cum[:, :, :, None, :] - cum[:, :, None, :, :]
        decay = jnp.exp(jnp.where(causal, diff, -jnp.inf))
        att = jnp.einsum('bhid,bhjd,bhijd->bhij', qc, kc, decay)
        o = att @ vc + jnp.einsum('bhid,bhde->bhie', qc * jnp.exp(cum), s)
        last = cum[:, :, -1:, :]
        s_new = jnp.exp(last[:, :, 0, :])[..., None] * s + jnp.einsum('bhjd,bhje->bhde', kc * jnp.exp(last - cum), vc)
        return s_new, o

    s_fin, o = lax.scan(step, s0.astype(jnp.float32), (to_chunks(q), to_chunks(k), to_chunks(v), to_chunks(logg)))
    o = o.transpose(1, 0, 3, 2, 4).reshape(b, t, h, dv)
    return o, s_fin


def even_mixer(h, w_in, w_a2, b_a, gla_norm_g, w_out, s0):
    b, t, _ = h.shape
    proj = h @ w_in
    u_f, q, k, v, g, a_lr = jnp.split(proj, EVEN_SPLITS, axis=-1)
    uf = u_f.astype(jnp.float32).reshape(b, t, FNET_GROUPS, FNET_GROUP_DIM)
    f_out = jnp.real(jnp.fft.fft2(uf, axes=(1, 3), norm='ortho')).reshape(b, t, FNET_WIDTH)
    qf = q.astype(jnp.float32).reshape(b, t, GLA_HEADS, GLA_DK_HEAD) * (GLA_DK_HEAD ** -0.5)
    kf = k.astype(jnp.float32).reshape(b, t, GLA_HEADS, GLA_DK_HEAD)
    vf = v.astype(jnp.float32).reshape(b, t, GLA_HEADS, GLA_DV_HEAD)
    a_lr = a_lr.reshape(b, t, 2, GLA_GATE_RANK)
    gate_logits = jnp.einsum('btzr,zre->btze', a_lr, w_a2) + b_a
    logg = (jax.nn.log_sigmoid(gate_logits.astype(jnp.float32)) / GLA_TAU).reshape(b, t, 2, GLA_HEADS, GLA_DK_HEAD)
    o_f, s_f = gla_chunk_scan(qf, kf, vf, logg[:, :, 0], s0[:, 0])
    flip = lambda a: jnp.flip(a, axis=1)
    o_b, s_b = gla_chunk_scan(flip(qf), flip(kf), flip(vf), flip(logg[:, :, 1]), s0[:, 1])
    o = o_f + flip(o_b)
    o = rms_norm(o, gla_norm_g).reshape(b, t, GLA_DV) * jax.nn.silu(g.astype(jnp.float32))
    mix = jnp.concatenate([f_out, o], axis=-1).astype(h.dtype)
    return mix @ w_out, jnp.stack([s_f, s_b], axis=1)


def odd_qkv(h, w_qkv, q_norm_g, k_norm_g):
    b, t, _ = h.shape
    q, k, v = jnp.split(h @ w_qkv, ODD_SPLITS, axis=-1)
    q = rms_norm(q.reshape(b, t, ATT_HEADS, HEAD_DIM), q_norm_g)
    k = rms_norm(k.reshape(b, t, ATT_KV_HEADS, HEAD_DIM), k_norm_g)
    v = v.reshape(b, t, ATT_KV_HEADS, HEAD_DIM)
    return q, k, v


def moe(h, w_router, b_router, w_gu, b_gu, w_down, b_down):
    b, t, d = h.shape
    xt = h.reshape(b * t, d)
    logits = (xt @ w_router + b_router).astype(jnp.float32)
    top_v, top_i = lax.top_k(logits, TOP_K)
    top_w = jax.nn.softmax(top_v, axis=-1)
    combine = jnp.einsum('nk,nke->ne', top_w, jax.nn.one_hot(top_i, N_EXPERTS, dtype=jnp.float32))
    y = jnp.zeros((b * t, d), jnp.float32)
    for e in range(N_EXPERTS):
        gu = xt @ w_gu[e] + b_gu[e]
        gate = jnp.minimum(gu[:, :D_FF], SWIGLU_LIMIT)
        up = jnp.clip(gu[:, D_FF:], -SWIGLU_LIMIT, SWIGLU_LIMIT)
        act = (up + 1.0) * gate * jax.nn.sigmoid(SWIGLU_ALPHA * gate)
        y = y + combine[:, e:e + 1] * (act @ w_down[e] + b_down[e])
    return y.reshape(b, t, d).astype(h.dtype)


def trunk(x, cmod, p, gla_init=None, ctx_k=None, ctx_v=None):
    latent = gla_init is not None
    b, t, _ = x.shape
    if latent:
        cos, sin = axial_rope(t)
    new_s, new_k, new_v = [], [], []
    for l in range(DEPTH):
        i = l // 2
        sh1, sc1, g1, sh2, sc2, g2 = modulation(cmod, p['w_mod'][l], p['b_mod'][l])
        h = layer_norm(x) * (1 + sc1) + sh1
        if l % 2 == 0:
            if latent:
                s0 = gla_init[:, i]
            else:
                s0 = jnp.zeros((b, 2, GLA_HEADS, GLA_DK_HEAD, GLA_DV_HEAD), jnp.float32)
            out, s_fin = even_mixer(h, p['w_in_even'][i], p['w_a2'][i], p['b_a'][i], p['gla_norm_g'][i], p['w_out_even'][i], s0)
            if not latent:
                new_s.append(s_fin.astype(x.dtype))
        else:
            q, k, v = odd_qkv(h, p['w_qkv'][i], p['q_norm_g'][i], p['k_norm_g'][i])
            if latent:
                q = apply_rope(q, cos, sin)
                k_rot = apply_rope(k, cos, sin)
                keys = jnp.concatenate([ctx_k[:, i].astype(k.dtype), k_rot], axis=1)
                vals = jnp.concatenate([ctx_v[:, i].astype(v.dtype), v], axis=1)
                o = block_attention(q, keys, vals)
            else:
                o = block_attention(q, k, v)
                new_k.append(k)
                new_v.append(v)
            out = o.reshape(b, t, D_MODEL) @ p['w_o'][i]
        x = layer_norm(DEEPNORM_ALPHA * x + g1 * out, p['ln_g'][l, 0], p['ln_b'][l, 0])
        h = layer_norm(x) * (1 + sc2) + sh2
        f = moe(h, p['w_router'][l], p['b_router'][l], p['w_gate_up'][l], p['b_gate_up'][l], p['w_down'][l], p['b_down'][l])
        x = layer_norm(DEEPNORM_ALPHA * x + g2 * f, p['ln_g'][l, 1], p['ln_b'][l, 1])
    return x, new_s, new_k, new_v


def setup_inputs(seed: int = 0) -> dict:
    key = jax.random.key(seed)
    ks = jax.random.split(key, 32)
    nrm = lambda k, shape, s: jax.random.normal(k, shape, jnp.float32) * s
    d = D_MODEL
    return {
        'x_prompt': nrm(ks[0], (BATCH, SEQ, d), 1.0),
        'x_sample': nrm(ks[1], (DEC_BATCH, DEC_SEQ, d), 1.0),
        'state_gla': nrm(ks[2], (DEC_BATCH, N_EVEN, 2, GLA_HEADS, GLA_DK_HEAD, GLA_DV_HEAD), 0.3),
        'cache_k': nrm(ks[3], (DEC_BATCH, N_ODD, PAST_LEN, ATT_KV_HEADS, HEAD_DIM), 1.0),
        'cache_v': nrm(ks[4], (DEC_BATCH, N_ODD, PAST_LEN, ATT_KV_HEADS, HEAD_DIM), 1.0),
        'c': nrm(ks[5], (DEC_BATCH, d), 1.0),
        'c_ctx': nrm(ks[6], (d,), 1.0),
        'w_mod': nrm(ks[7], (DEPTH, d, 6 * d), 0.5 * d ** -0.5),
        'b_mod': nrm(ks[8], (DEPTH, 6 * d), 0.01),
        'ln_g': 1.0 + nrm(ks[9], (DEPTH, 2, d), 0.02),
        'ln_b': nrm(ks[10], (DEPTH, 2, d), 0.02),
        'w_in_even': nrm(ks[11], (N_EVEN, d, EVEN_IN), d ** -0.5),
        'w_a2': nrm(ks[12], (N_EVEN, 2, GLA_GATE_RANK, GLA_DK), GLA_GATE_RANK ** -0.5),
        'b_a': nrm(ks[13], (N_EVEN, 2, GLA_DK), 0.1),
        'gla_norm_g': 1.0 + nrm(ks[14], (N_EVEN, GLA_DV_HEAD), 0.02),
        'w_out_even': nrm(ks[15], (N_EVEN, d, d), DEEPNORM_BETA * d ** -0.5),
        'w_qkv': nrm(ks[16], (N_ODD, d, ODD_IN), d ** -0.5),
        'q_norm_g': 1.0 + nrm(ks[17], (N_ODD, HEAD_DIM), 0.02),
        'k_norm_g': 1.0 + nrm(ks[18], (N_ODD, HEAD_DIM), 0.02),
        'w_o': nrm(ks[19], (N_ODD, d, d), DEEPNORM_BETA * d ** -0.5),
        'w_router': nrm(ks[20], (DEPTH, d, N_EXPERTS), d ** -0.5),
        'b_router': nrm(ks[21], (DEPTH, N_EXPERTS), 0.01),
        'w_gate_up': nrm(ks[22], (DEPTH, N_EXPERTS, d, 2 * D_FF), d ** -0.5),
        'b_gate_up': nrm(ks[23], (DEPTH, N_EXPERTS, 2 * D_FF), 0.01),
        'w_down': nrm(ks[24], (DEPTH, N_EXPERTS, D_FF, d), DEEPNORM_BETA * D_FF ** -0.5),
        'b_down': nrm(ks[25], (DEPTH, N_EXPERTS, d), 0.01),
    }


def reference(x_prompt, x_sample, state_gla, cache_k, cache_v, c, c_ctx, w_mod, b_mod, ln_g, ln_b,
              w_in_even, w_a2, b_a, gla_norm_g, w_out_even, w_qkv, q_norm_g, k_norm_g, w_o,
              w_router, b_router, w_gate_up, b_gate_up, w_down, b_down):
    p = dict(w_mod=w_mod, b_mod=b_mod, ln_g=ln_g, ln_b=ln_b, w_in_even=w_in_even, w_a2=w_a2, b_a=b_a,
             gla_norm_g=gla_norm_g, w_out_even=w_out_even, w_qkv=w_qkv, q_norm_g=q_norm_g,
             k_norm_g=k_norm_g, w_o=w_o, w_router=w_router, b_router=b_router, w_gate_up=w_gate_up,
             b_gate_up=b_gate_up, w_down=w_down, b_down=b_down)
    y_prompt, s_list, k_list, v_list = trunk(x_prompt, c_ctx[None, :], p)
    new_state_gla = jnp.stack(s_list, axis=1)
    new_cache_k = jnp.stack(k_list, axis=1)
    new_cache_v = jnp.stack(v_list, axis=1)
    y_sample, _, _, _ = trunk(x_sample, c, p, state_gla, cache_k, cache_v)
    return (y_prompt, y_sample, new_state_gla, new_cache_k, new_cache_v)
```

```python
import jax
import jax.numpy as jnp
from jax import lax
from jax.experimental import pallas as pl
from jax.experimental.pallas import tpu as pltpu

F32 = jnp.float32
BF16 = jnp.bfloat16
I32 = jnp.int32

D_MODEL = 1024
BATCH, SEQ = 16, 256
DEC_BATCH, DEC_SEQ = 4, 2048
DEPTH = 4
GRID_W = 64
FNET_WIDTH, FNET_GROUPS, FNET_GROUP_DIM = 512, 4, 128
GLA_HEADS, GLA_DK, GLA_DV = 4, 256, 512
GLA_DK_HEAD, GLA_DV_HEAD = 64, 128
GLA_GATE_RANK, GLA_TAU, GLA_CHUNK = 16, 16.0, 64
EVEN_SPLITS = (512, 768, 1024, 1536, 2048)
ATT_HEADS, ATT_KV_HEADS, HEAD_DIM = 8, 2, 128
ROPE_THETA = 10000.0
ROPE_PAIRS_PER_AXIS = HEAD_DIM // 4
Q_BLOCK = 128
ODD_SPLITS = (ATT_HEADS * HEAD_DIM, (ATT_HEADS + ATT_KV_HEADS) * HEAD_DIM)
N_EXPERTS, TOP_K, D_FF = 32, 4, 1024
SWIGLU_LIMIT, SWIGLU_ALPHA = 7.0, 1.702
LN_EPS, RMS_EPS = 1e-5, 1e-6
DEEPNORM_ALPHA = (2.0 * DEPTH) ** 0.25

N_CTX = BATCH * SEQ
N_LAT = DEC_BATCH * DEC_SEQ
N_TOK = N_CTX + N_LAT
N_ASSIGN = N_TOK * TOP_K
TM = 256
ROWS_MAX = N_ASSIGN + N_EXPERTS * TM
NT_EXP = ROWS_MAX // TM
VMEM_LIMIT = 48 * 1024 * 1024


def _expert_kernel(te_ref, used_ref, xs_ref, wgu_ref, bgu_ref, wd_ref, bd_ref, y_ref):
    del te_ref
    t = pl.program_id(0)

    @pl.when(t < used_ref[0])
    def _():
        gu = jnp.dot(xs_ref[...], wgu_ref[0], preferred_element_type=F32) + bgu_ref[0]
        gate = jnp.minimum(gu[:, :D_FF], SWIGLU_LIMIT)
        up = jnp.clip(gu[:, D_FF:], -SWIGLU_LIMIT, SWIGLU_LIMIT)
        act = (up + 1.0) * gate * jax.nn.sigmoid(SWIGLU_ALPHA * gate)
        y_ref[...] = jnp.dot(act.astype(BF16), wd_ref[0], preferred_element_type=F32) + bd_ref[0]

    @pl.when(t >= used_ref[0])
    def _():
        y_ref[...] = jnp.zeros_like(y_ref)


def expert_mlp(tile_expert, n_used, xs, wgu, bgu, wd, bd):
    grid_spec = pltpu.PrefetchScalarGridSpec(
        num_scalar_prefetch=2,
        grid=(NT_EXP,),
        in_specs=[
            pl.BlockSpec((TM, D_MODEL), lambda t, te, nu: (t, 0)),
            pl.BlockSpec((1, D_MODEL, 2 * D_FF), lambda t, te, nu: (te[t], 0, 0)),
            pl.BlockSpec((1, 1, 2 * D_FF), lambda t, te, nu: (te[t], 0, 0)),
            pl.BlockSpec((1, D_FF, D_MODEL), lambda t, te, nu: (te[t], 0, 0)),
            pl.BlockSpec((1, 1, D_MODEL), lambda t, te, nu: (te[t], 0, 0)),
        ],
        out_specs=pl.BlockSpec((TM, D_MODEL), lambda t, te, nu: (t, 0)),
    )
    return pl.pallas_call(
        _expert_kernel,
        grid_spec=grid_spec,
        out_shape=jax.ShapeDtypeStruct((ROWS_MAX, D_MODEL), F32),
        compiler_params=pltpu.CompilerParams(dimension_semantics=("arbitrary",),
                                             vmem_limit_bytes=VMEM_LIMIT),
        name="expert_mlp",
    )(tile_expert, n_used, xs, wgu, bgu, wd, bd)


def moe(h, w_router, b_router, wgu_bf, b_gu, wd_bf, b_down):
    logits = (h @ w_router + b_router).astype(F32)
    top_v, top_i = lax.top_k(logits, TOP_K)
    top_w = jax.nn.softmax(top_v, axis=-1)

    flat_e = top_i.reshape(-1).astype(I32)
    order = jnp.argsort(flat_e, stable=True).astype(I32)
    counts = jnp.sum(jax.nn.one_hot(flat_e, N_EXPERTS, dtype=I32), axis=0)
    cnt_pad = ((counts + TM - 1) // TM) * TM
    end_pad = jnp.cumsum(cnt_pad)
    start_pad = end_pad - cnt_pad
    start_dense = jnp.cumsum(counts) - counts
    total = end_pad[-1]
    n_used = (total // TM).astype(I32)
    tile_row = jnp.arange(NT_EXP, dtype=I32) * TM
    tile_expert = jnp.minimum(jnp.sum((tile_row[:, None] >= end_pad[None, :]).astype(I32), axis=1),
                              N_EXPERTS - 1)
    tile_expert = jnp.where(tile_row < total, tile_expert, tile_expert[n_used - 1]).astype(I32)
    rows = jnp.arange(ROWS_MAX, dtype=I32)
    e_row = jnp.repeat(tile_expert, TM)
    j = rows - start_pad[e_row]
    valid = jnp.logical_and(j < counts[e_row], rows < total)
    assign = order[jnp.clip(start_dense[e_row] + j, 0, N_ASSIGN - 1)]
    src_tok = jnp.where(valid, assign // TOP_K, 0)
    xs = h.astype(BF16)[src_tok]

    y = expert_mlp(tile_expert, n_used.reshape(1), xs, wgu_bf, b_gu.reshape(N_EXPERTS, 1, 2 * D_FF),
                   wd_bf, b_down.reshape(N_EXPERTS, 1, D_MODEL))

    inv = jnp.zeros((N_ASSIGN,), I32).at[order].set(jnp.arange(N_ASSIGN, dtype=I32), unique_indices=True)
    row_of = (start_pad[flat_e] + inv - start_dense[flat_e]).reshape(N_TOK, TOP_K)
    out = top_w[:, 0:1] * y[row_of[:, 0]]
    for kk in range(1, TOP_K):
        out = out + top_w[:, kk:kk + 1] * y[row_of[:, kk]]
    return out


def layer_norm(x, g=None, b=None):
    mu = jnp.mean(x, axis=-1, keepdims=True)
    var = jnp.mean(jnp.square(x - mu), axis=-1, keepdims=True)
    y = (x - mu) * lax.rsqrt(var + LN_EPS)
    if g is not None:
        y = y * g + b
    return y


def rms_norm(x, g):
    return x * lax.rsqrt(jnp.mean(jnp.square(x), axis=-1, keepdims=True) + RMS_EPS) * g


def modulation(cmod, w_mod_l, b_mod_l):
    m = jax.nn.silu(cmod) @ w_mod_l + b_mod_l
    return jnp.split(m[:, None, :], 6, axis=-1)


def axial_rope(n_tok):
    n_rows = n_tok // GRID_W
    row = jnp.broadcast_to(jnp.arange(n_rows, dtype=F32)[:, None], (n_rows, GRID_W)).reshape(-1)
    col = jnp.broadcast_to(jnp.arange(GRID_W, dtype=F32)[None, :], (n_rows, GRID_W)).reshape(-1)
    freqs = ROPE_THETA ** (-jnp.arange(ROPE_PAIRS_PER_AXIS, dtype=F32) / ROPE_PAIRS_PER_AXIS)
    ang = jnp.concatenate([row[:, None] * freqs, col[:, None] * freqs], axis=-1)
    return jnp.cos(ang), jnp.sin(ang)


def apply_rope(x, cos, sin):
    b, t, h, d = x.shape
    xp = x.reshape(b, t, h, d // 2, 2)
    x0, x1 = xp[..., 0], xp[..., 1]
    c = cos[None, :, None, :]
    s = sin[None, :, None, :]
    return jnp.stack([x0 * c - x1 * s, x0 * s + x1 * c], axis=-1).reshape(b, t, h, d)


def block_attention(q, k, v):
    b, tq, h, hd = q.shape
    g = h // ATT_KV_HEADS
    nb = tq // Q_BLOCK
    qb = q.reshape(b, nb, Q_BLOCK, ATT_KV_HEADS, g, hd).transpose(1, 0, 2, 3, 4, 5)
    scale = hd ** -0.5

    def one_block(qblk):
        s = jnp.einsum('bqkgd,bskd->bkgqs', qblk, k, preferred_element_type=F32) * scale
        p = jax.nn.softmax(s, axis=-1)
        return jnp.einsum('bkgqs,bskd->bqkgd', p, v)

    o = lax.map(one_block, qb)
    return o.transpose(1, 0, 2, 3, 4, 5).reshape(b, tq, h, hd)


def gla_chunk_scan(q, k, v, logg, s0):
    b, t, h, _ = q.shape
    dv = v.shape[-1]
    n = t // GLA_CHUNK

    def to_chunks(a):
        return a.reshape(b, n, GLA_CHUNK, h, a.shape[-1]).transpose(1, 0, 3, 2, 4)

    causal = jnp.tril(jnp.ones((GLA_CHUNK, GLA_CHUNK), dtype=bool))[:, :, None]

    def step(s, inp):
        qc, kc, vc, gc = inp
        cum = jnp.cumsum(gc, axis=-2)
        diff = cum[:, :, :, None, :] - cum[:, :, None, :, :]
        decay = jnp.exp(jnp.where(causal, diff, -jnp.inf))
        att = jnp.einsum('bhid,bhjd,bhijd->bhij', qc, kc, decay)
        o = att @ vc + jnp.einsum('bhid,bhde->bhie', qc * jnp.exp(cum), s)
        last = cum[:, :, -1:, :]
        s_new = (jnp.exp(last[:, :, 0, :])[..., None] * s
                 + jnp.einsum('bhjd,bhje->bhde', kc * jnp.exp(last - cum), vc))
        return s_new, o

    s_fin, o = lax.scan(step, s0, (to_chunks(q), to_chunks(k), to_chunks(v), to_chunks(logg)))
    o = o.transpose(1, 0, 3, 2, 4).reshape(b, t, h, dv)
    return o, s_fin


def even_mixer(h, w_in, w_a2, b_a, gla_norm_g, w_out, s0):
    b, t, _ = h.shape
    proj = h @ w_in
    u_f, q, k, v, g, a_lr = jnp.split(proj, EVEN_SPLITS, axis=-1)
    uf = u_f.reshape(b, t, FNET_GROUPS, FNET_GROUP_DIM)
    f_out = jnp.real(jnp.fft.fft2(uf, axes=(1, 3), norm='ortho')).reshape(b, t, FNET_WIDTH)
    qf = q.reshape(b, t, GLA_HEADS, GLA_DK_HEAD) * (GLA_DK_HEAD ** -0.5)
    kf = k.reshape(b, t, GLA_HEADS, GLA_DK_HEAD)
    vf = v.reshape(b, t, GLA_HEADS, GLA_DV_HEAD)
    a_lr = a_lr.reshape(b, t, 2, GLA_GATE_RANK)
    gate_logits = jnp.einsum('btzr,zre->btze', a_lr, w_a2) + b_a
    logg = (jax.nn.log_sigmoid(gate_logits) / GLA_TAU).reshape(b, t, 2, GLA_HEADS, GLA_DK_HEAD)
    o_f, s_f = gla_chunk_scan(qf, kf, vf, logg[:, :, 0], s0[:, 0])
    flip = lambda a: jnp.flip(a, axis=1)
    o_b, s_b = gla_chunk_scan(flip(qf), flip(kf), flip(vf), flip(logg[:, :, 1]), s0[:, 1])
    o = o_f + flip(o_b)
    o = rms_norm(o, gla_norm_g).reshape(b, t, GLA_DV) * jax.nn.silu(g)
    mix = jnp.concatenate([f_out, o], axis=-1)
    return mix @ w_out, jnp.stack([s_f, s_b], axis=1)


def odd_qkv(h, w_qkv, q_norm_g, k_norm_g):
    b, t, _ = h.shape
    q, k, v = jnp.split(h @ w_qkv, ODD_SPLITS, axis=-1)
    q = rms_norm(q.reshape(b, t, ATT_HEADS, HEAD_DIM), q_norm_g)
    k = rms_norm(k.reshape(b, t, ATT_KV_HEADS, HEAD_DIM), k_norm_g)
    v = v.reshape(b, t, ATT_KV_HEADS, HEAD_DIM)
    return q, k, v


def kernel(x_prompt, x_sample, state_gla, cache_k, cache_v, c, c_ctx, w_mod, b_mod, ln_g, ln_b,
           w_in_even, w_a2, b_a, gla_norm_g, w_out_even, w_qkv, q_norm_g, k_norm_g, w_o,
           w_router, b_router, w_gate_up, b_gate_up, w_down, b_down):
    wgu_bf = w_gate_up.astype(BF16)
    wd_bf = w_down.astype(BF16)
    cos, sin = axial_rope(DEC_SEQ)
    xs = [x_prompt, x_sample]
    cmods = [c_ctx[None, :], c]
    new_s, new_k, new_v = [], [], []
    for l in range(DEPTH):
        i = l // 2
        mods = [modulation(cm, w_mod[l], b_mod[l]) for cm in cmods]
        for s_idx in range(2):
            x = xs[s_idx]
            latent = s_idx == 1
            sh1, sc1, g1 = mods[s_idx][0], mods[s_idx][1], mods[s_idx][2]
            h = layer_norm(x) * (1 + sc1) + sh1
            if l % 2 == 0:
                if latent:
                    s0 = state_gla[:, i]
                else:
                    s0 = jnp.zeros((BATCH, 2, GLA_HEADS, GLA_DK_HEAD, GLA_DV_HEAD), F32)
                out, s_fin = even_mixer(h, w_in_even[i], w_a2[i], b_a[i], gla_norm_g[i], w_out_even[i], s0)
                if not latent:
                    new_s.append(s_fin)
            else:
                q, k, v = odd_qkv(h, w_qkv[i], q_norm_g[i], k_norm_g[i])
                if latent:
                    q = apply_rope(q, cos, sin)
                    k_rot = apply_rope(k, cos, sin)
                    keys = jnp.concatenate([cache_k[:, i], k_rot], axis=1)
                    vals = jnp.concatenate([cache_v[:, i], v], axis=1)
                    o = block_attention(q, keys, vals)
                else:
                    o = block_attention(q, k, v)
                    new_k.append(k)
                    new_v.append(v)
                out = o.reshape(x.shape) @ w_o[i]
            xs[s_idx] = layer_norm(DEEPNORM_ALPHA * x + g1 * out, ln_g[l, 0], ln_b[l, 0])
        hs = []
        for s_idx in range(2):
            sh2, sc2 = mods[s_idx][3], mods[s_idx][4]
            hs.append((layer_norm(xs[s_idx]) * (1 + sc2) + sh2).reshape(-1, D_MODEL))
        f = moe(jnp.concatenate(hs, axis=0), w_router[l], b_router[l], wgu_bf[l], b_gate_up[l],
                wd_bf[l], b_down[l])
        fs = [f[:N_CTX].reshape(BATCH, SEQ, D_MODEL), f[N_CTX:].reshape(DEC_BATCH, DEC_SEQ, D_MODEL)]
        for s_idx in range(2):
            g2 = mods[s_idx][5]
            xs[s_idx] = layer_norm(DEEPNORM_ALPHA * xs[s_idx] + g2 * fs[s_idx], ln_g[l, 1], ln_b[l, 1])
    return (xs[0], xs[1], jnp.stack(new_s, axis=1), jnp.stack(new_k, axis=1), jnp.stack(new_v, axis=1))
```

```python
import functools

import jax
import jax.numpy as jnp
from jax import lax
from jax.experimental import pallas as pl
from jax.experimental.pallas import tpu as pltpu

F32 = jnp.float32
BF16 = jnp.bfloat16
I32 = jnp.int32

D_MODEL = 1024
BATCH, SEQ = 16, 256
DEC_BATCH, DEC_SEQ = 4, 2048
DEPTH = 4
GRID_W = 64
FNET_WIDTH, FNET_GROUPS, FNET_GROUP_DIM = 512, 4, 128
GLA_HEADS, GLA_DK, GLA_DV = 4, 256, 512
GLA_DK_HEAD, GLA_DV_HEAD = 64, 128
GLA_GATE_RANK, GLA_TAU, GLA_CHUNK = 16, 16.0, 64
EVEN_SPLITS = (512, 768, 1024, 1536, 2048)
ATT_HEADS, ATT_KV_HEADS, HEAD_DIM = 8, 2, 128
ROPE_THETA = 10000.0
ROPE_PAIRS_PER_AXIS = HEAD_DIM // 4
Q_BLOCK = 128
ODD_SPLITS = (ATT_HEADS * HEAD_DIM, (ATT_HEADS + ATT_KV_HEADS) * HEAD_DIM)
N_EXPERTS, TOP_K, D_FF = 32, 4, 1024
SWIGLU_LIMIT, SWIGLU_ALPHA = 7.0, 1.702
LN_EPS, RMS_EPS = 1e-5, 1e-6
DEEPNORM_ALPHA = (2.0 * DEPTH) ** 0.25

N_CTX = BATCH * SEQ
N_LAT = DEC_BATCH * DEC_SEQ
N_TOK = N_CTX + N_LAT
N_ASSIGN = N_TOK * TOP_K
TM = 256
ROWS_MAX = N_ASSIGN + N_EXPERTS * TM
NT_EXP = ROWS_MAX // TM
VMEM_LIMIT = 48 * 1024 * 1024


def _expert_kernel(te_ref, used_ref, xs_ref, wgu_ref, bgu_ref, wd_ref, bd_ref, y_ref):
    del te_ref
    t = pl.program_id(0)

    @pl.when(t < used_ref[0])
    def _():
        gu = jnp.dot(xs_ref[...], wgu_ref[0], preferred_element_type=F32) + bgu_ref[0]
        gate = jnp.minimum(gu[:, :D_FF], SWIGLU_LIMIT)
        up = jnp.clip(gu[:, D_FF:], -SWIGLU_LIMIT, SWIGLU_LIMIT)
        act = (up + 1.0) * gate * jax.nn.sigmoid(SWIGLU_ALPHA * gate)
        y_ref[...] = jnp.dot(act.astype(BF16), wd_ref[0], preferred_element_type=F32) + bd_ref[0]

    @pl.when(t >= used_ref[0])
    def _():
        y_ref[...] = jnp.zeros_like(y_ref)


def expert_mlp(tile_expert, n_used, xs, wgu, bgu, wd, bd):
    grid_spec = pltpu.PrefetchScalarGridSpec(
        num_scalar_prefetch=2,
        grid=(NT_EXP,),
        in_specs=[
            pl.BlockSpec((TM, D_MODEL), lambda t, te, nu: (t, 0)),
            pl.BlockSpec((1, D_MODEL, 2 * D_FF), lambda t, te, nu: (te[t], 0, 0)),
            pl.BlockSpec((1, 1, 2 * D_FF), lambda t, te, nu: (te[t], 0, 0)),
            pl.BlockSpec((1, D_FF, D_MODEL), lambda t, te, nu: (te[t], 0, 0)),
            pl.BlockSpec((1, 1, D_MODEL), lambda t, te, nu: (te[t], 0, 0)),
        ],
        out_specs=pl.BlockSpec((TM, D_MODEL), lambda t, te, nu: (t, 0)),
    )
    return pl.pallas_call(
        _expert_kernel,
        grid_spec=grid_spec,
        out_shape=jax.ShapeDtypeStruct((ROWS_MAX, D_MODEL), F32),
        compiler_params=pltpu.CompilerParams(dimension_semantics=("arbitrary",),
                                             vmem_limit_bytes=VMEM_LIMIT),
        name="expert_mlp",
    )(tile_expert, n_used, xs, wgu, bgu, wd, bd)


def moe(h, w_router, b_router, wgu_bf, b_gu, wd_bf, b_down):
    logits = (h @ w_router + b_router).astype(F32)
    top_v, top_i = lax.top_k(logits, TOP_K)
    top_w = jax.nn.softmax(top_v, axis=-1)

    flat_e = top_i.reshape(-1).astype(I32)
    order = jnp.argsort(flat_e, stable=True).astype(I32)
    counts = jnp.sum(jax.nn.one_hot(flat_e, N_EXPERTS, dtype=I32), axis=0)
    cnt_pad = ((counts + TM - 1) // TM) * TM
    end_pad = jnp.cumsum(cnt_pad)
    start_pad = end_pad - cnt_pad
    start_dense = jnp.cumsum(counts) - counts
    total = end_pad[-1]
    n_used = (total // TM).astype(I32)
    tile_row = jnp.arange(NT_EXP, dtype=I32) * TM
    tile_expert = jnp.minimum(jnp.sum((tile_row[:, None] >= end_pad[None, :]).astype(I32), axis=1),
                              N_EXPERTS - 1)
    tile_expert = jnp.where(tile_row < total, tile_expert, tile_expert[n_used - 1]).astype(I32)
    rows = jnp.arange(ROWS_MAX, dtype=I32)
    e_row = jnp.repeat(tile_expert, TM)
    j = rows - start_pad[e_row]
    valid = jnp.logical_and(j < counts[e_row], rows < total)
    assign = order[jnp.clip(start_dense[e_row] + j, 0, N_ASSIGN - 1)]
    src_tok = jnp.where(valid, assign // TOP_K, 0)
    xs = h.astype(BF16)[src_tok]

    y = expert_mlp(tile_expert, n_used.reshape(1), xs, wgu_bf, b_gu.reshape(N_EXPERTS, 1, 2 * D_FF),
                   wd_bf, b_down.reshape(N_EXPERTS, 1, D_MODEL))

    inv = jnp.zeros((N_ASSIGN,), I32).at[order].set(jnp.arange(N_ASSIGN, dtype=I32), unique_indices=True)
    row_of = (start_pad[flat_e] + inv - start_dense[flat_e]).reshape(N_TOK, TOP_K)
    out = top_w[:, 0:1] * y[row_of[:, 0]]
    for kk in range(1, TOP_K):
        out = out + top_w[:, kk:kk + 1] * y[row_of[:, kk]]
    return out


def layer_norm(x, g=None, b=None):
    mu = jnp.mean(x, axis=-1, keepdims=True)
    var = jnp.mean(jnp.square(x - mu), axis=-1, keepdims=True)
    y = (x - mu) * lax.rsqrt(var + LN_EPS)
    if g is not None:
        y = y * g + b
    return y


def rms_norm(x, g):
    return x * lax.rsqrt(jnp.mean(jnp.square(x), axis=-1, keepdims=True) + RMS_EPS) * g


def modulation(cmod, w_mod_l, b_mod_l):
    m = jax.nn.silu(cmod) @ w_mod_l + b_mod_l
    return jnp.split(m[:, None, :], 6, axis=-1)


def axial_rope(n_tok):
    n_rows = n_tok // GRID_W
    row = jnp.broadcast_to(jnp.arange(n_rows, dtype=F32)[:, None], (n_rows, GRID_W)).reshape(-1)
    col = jnp.broadcast_to(jnp.arange(GRID_W, dtype=F32)[None, :], (n_rows, GRID_W)).reshape(-1)
    freqs = ROPE_THETA ** (-jnp.arange(ROPE_PAIRS_PER_AXIS, dtype=F32) / ROPE_PAIRS_PER_AXIS)
    ang = jnp.concatenate([row[:, None] * freqs, col[:, None] * freqs], axis=-1)
    return jnp.cos(ang), jnp.sin(ang)


def apply_rope(x, cos, sin):
    b, t, h, d = x.shape
    xp = x.reshape(b, t, h, d // 2, 2)
    x0, x1 = xp[..., 0], xp[..., 1]
    c = cos[None, :, None, :]
    s = sin[None, :, None, :]
    return jnp.stack([x0 * c - x1 * s, x0 * s + x1 * c], axis=-1).reshape(b, t, h, d)


def block_attention(q, k, v):
    b, tq, h, hd = q.shape
    g = h // ATT_KV_HEADS
    nb = tq // Q_BLOCK
    qb = q.reshape(b, nb, Q_BLOCK, ATT_KV_HEADS, g, hd).transpose(1, 0, 2, 3, 4, 5)
    scale = hd ** -0.5

    def one_block(qblk):
        s = jnp.einsum('bqkgd,bskd->bkgqs', qblk, k, preferred_element_type=F32) * scale
        p = jax.nn.softmax(s, axis=-1)
        return jnp.einsum('bkgqs,bskd->bqkgd', p, v)

    o = lax.map(one_block, qb)
    return o.transpose(1, 0, 2, 3, 4, 5).reshape(b, tq, h, hd)


GT = 256


def _dot(a, b, **kw):
    return jnp.dot(a, b, preferred_element_type=F32, **kw)


def _dot_nt(a, b):
    return lax.dot_general(a, b, (((1,), (1,)), ((), ())), preferred_element_type=F32)


def _gla_tile(q, k, v, lg, st_ref, fwd):
    nb = GT // GLA_CHUNK
    row = lax.broadcasted_iota(I32, (GT, GT), 0)
    col = lax.broadcasted_iota(I32, (GT, GT), 1)
    causal = (col <= row) if fwd else (col >= row)
    cum = _dot(jnp.where(causal, 1.0, 0.0), lg, precision=lax.Precision.HIGHEST)

    def edge(c):
        r = GLA_CHUNK * c + (GLA_CHUNK - 1 if fwd else 0)
        return cum[r:r + 1, :]

    def mid(c):
        r = GLA_CHUNK * c + GLA_CHUNK // 2
        return cum[r:r + 1, :]

    total = cum[GT - 1:GT, :] if fwd else cum[0:1, :]
    shape = (GLA_CHUNK, GLA_DK)
    e_own = jnp.concatenate([jnp.broadcast_to(edge(c), shape) for c in range(nb)], axis=0)
    m_own = jnp.concatenate([jnp.broadcast_to(mid(c), shape) for c in range(nb)], axis=0)
    rowi = lax.broadcasted_iota(I32, (GT, 1), 0)
    lane = lax.broadcasted_iota(I32, (1, GLA_DK), 1)

    k_edge = k * jnp.exp(e_own - cum)
    q_mid = q * jnp.exp(cum - m_own)
    k_mid = (k * jnp.exp(m_own - cum)).astype(BF16)
    q_in = (q * jnp.exp(cum)).astype(BF16)
    k_out = k * jnp.exp(total - cum)

    srcs = list(range(nb - 1)) if fwd else list(range(nb - 1, 0, -1))
    q_cross, k_cross = [], []
    for c in srcs:
        later = (rowi >= GLA_CHUNK * (c + 1)) if fwd else (rowi < GLA_CHUNK * c)
        in_c = jnp.logical_and(rowi >= GLA_CHUNK * c, rowi < GLA_CHUNK * (c + 1))
        q_cross.append(jnp.where(later, q * jnp.exp(jnp.minimum(cum - edge(c), 0.0)), 0.0))
        k_cross.append(jnp.where(in_c, k_edge, 0.0).astype(BF16))

    same_blk = jnp.right_shift(row, 6) == jnp.right_shift(col, 6)
    inner_mask = jnp.logical_and(same_blk, causal)
    vb = v.astype(BF16)
    st = st_ref[...]
    outs = []
    for h in range(GLA_HEADS):
        in_h = jnp.logical_and(lane >= GLA_DK_HEAD * h, lane < GLA_DK_HEAD * (h + 1))
        s = jnp.where(inner_mask, _dot_nt(jnp.where(in_h, q_mid, 0.0).astype(BF16), k_mid), 0.0)
        for qc, kc in zip(q_cross, k_cross):
            s = s + _dot_nt(jnp.where(in_h, qc, 0.0).astype(BF16), kc)
        vh = vb[:, h * GLA_DV_HEAD:(h + 1) * GLA_DV_HEAD]
        outs.append(_dot(s.astype(BF16), vh))
    o = jnp.concatenate(outs, axis=1) + _dot(q_in, st.astype(BF16))

    tot_col = jnp.sum(jnp.where(row == col, jnp.broadcast_to(total, (GT, GLA_DK)), 0.0),
                      axis=1, keepdims=True)
    kv = _dot(jnp.transpose(k_out).astype(BF16), vb)
    r_head = jnp.right_shift(lax.broadcasted_iota(I32, (GLA_DK, GLA_DV), 0), 6)
    c_head = jnp.right_shift(lax.broadcasted_iota(I32, (GLA_DK, GLA_DV), 1), 7)
    st_ref[...] = jnp.exp(tot_col) * st + jnp.where(r_head == c_head, kv, 0.0)
    return o


def _gla_kernel(qk_ref, v_ref, lg_ref, s0_ref, o_ref, sfin_ref, st_ref, *, nt):
    d = pl.program_id(1)
    j = pl.program_id(2)

    @pl.when(j == 0)
    def _():
        st_ref[...] = jnp.zeros_like(st_ref)
        for h in range(GLA_HEADS):
            st_ref[h * GLA_DK_HEAD:(h + 1) * GLA_DK_HEAD,
                   h * GLA_DV_HEAD:(h + 1) * GLA_DV_HEAD] = s0_ref[0, 0, h]

    for fwd in (True, False):
        @pl.when(d == (0 if fwd else 1))
        def _():
            q = qk_ref[:, :GLA_DK] * (GLA_DK_HEAD ** -0.5)
            k = qk_ref[:, GLA_DK:]
            o_ref[0] = _gla_tile(q, k, v_ref[...], lg_ref[...], st_ref, fwd)

    @pl.when(j == nt - 1)
    def _():
        for h in range(GLA_HEADS):
            sfin_ref[0, 0, h] = st_ref[h * GLA_DK_HEAD:(h + 1) * GLA_DK_HEAD,
                                       h * GLA_DV_HEAD:(h + 1) * GLA_DV_HEAD]


def gla_scan(qk, v, logg, s0, *, seq, n_batch):
    nt = seq // GT
    rows = n_batch * seq

    def tile(b, d, j):
        return b * nt + jnp.where(d == 0, j, nt - 1 - j)

    state_spec = pl.BlockSpec((1, 1, GLA_HEADS, GLA_DK_HEAD, GLA_DV_HEAD), lambda b, d, j: (b, d, 0, 0, 0))
    return pl.pallas_call(
        functools.partial(_gla_kernel, nt=nt),
        grid=(n_batch, 2, nt),
        in_specs=[
            pl.BlockSpec((GT, 2 * GLA_DK), lambda b, d, j: (tile(b, d, j), 0)),
            pl.BlockSpec((GT, GLA_DV), lambda b, d, j: (tile(b, d, j), 0)),
            pl.BlockSpec((GT, GLA_DK), lambda b, d, j: (tile(b, d, j), d)),
            state_spec,
        ],
        out_specs=[
            pl.BlockSpec((1, GT, GLA_DV), lambda b, d, j: (d, tile(b, d, j), 0)),
            state_spec,
        ],
        out_shape=[
            jax.ShapeDtypeStruct((2, rows, GLA_DV), F32),
            jax.ShapeDtypeStruct((n_batch, 2, GLA_HEADS, GLA_DK_HEAD, GLA_DV_HEAD), F32),
        ],
        scratch_shapes=[pltpu.VMEM((GLA_DK, GLA_DV), F32)],
        compiler_params=pltpu.CompilerParams(dimension_semantics=("arbitrary", "arbitrary", "arbitrary"),
                                             vmem_limit_bytes=VMEM_LIMIT),
        name=f"gla_scan_{seq}",
    )(qk, v, logg, s0)


def even_mixer(h, w_in, w_a2, b_a, gla_norm_g, w_out, s0):
    b, t, _ = h.shape
    proj = h @ w_in
    u_f, q, k, v, g, a_lr = jnp.split(proj, EVEN_SPLITS, axis=-1)
    uf = u_f.reshape(b, t, FNET_GROUPS, FNET_GROUP_DIM)
    f_out = jnp.real(jnp.fft.fft2(uf, axes=(1, 3), norm='ortho')).reshape(b, t, FNET_WIDTH)
    a_lr = a_lr.reshape(b, t, 2, GLA_GATE_RANK)
    gate_logits = jnp.einsum('btzr,zre->btze', a_lr, w_a2) + b_a
    logg = (jax.nn.log_sigmoid(gate_logits) / GLA_TAU).reshape(b * t, 2 * GLA_DK)
    qk = jnp.concatenate([q, k], axis=-1).reshape(b * t, 2 * GLA_DK)
    o2, s_fin = gla_scan(qk, v.reshape(b * t, GLA_DV), logg, s0, seq=t, n_batch=b)
    o = (o2[0] + o2[1]).reshape(b, t, GLA_HEADS, GLA_DV_HEAD)
    o = rms_norm(o, gla_norm_g).reshape(b, t, GLA_DV) * jax.nn.silu(g)
    mix = jnp.concatenate([f_out, o], axis=-1)
    return mix @ w_out, s_fin


def odd_qkv(h, w_qkv, q_norm_g, k_norm_g):
    b, t, _ = h.shape
    q, k, v = jnp.split(h @ w_qkv, ODD_SPLITS, axis=-1)
    q = rms_norm(q.reshape(b, t, ATT_HEADS, HEAD_DIM), q_norm_g)
    k = rms_norm(k.reshape(b, t, ATT_KV_HEADS, HEAD_DIM), k_norm_g)
    v = v.reshape(b, t, ATT_KV_HEADS, HEAD_DIM)
    return q, k, v


def kernel(x_prompt, x_sample, state_gla, cache_k, cache_v, c, c_ctx, w_mod, b_mod, ln_g, ln_b,
           w_in_even, w_a2, b_a, gla_norm_g, w_out_even, w_qkv, q_norm_g, k_norm_g, w_o,
           w_router, b_router, w_gate_up, b_gate_up, w_down, b_down):
    wgu_bf = w_gate_up.astype(BF16)
    wd_bf = w_down.astype(BF16)
    cos, sin = axial_rope(DEC_SEQ)
    xs = [x_prompt, x_sample]
    cmods = [c_ctx[None, :], c]
    new_s, new_k, new_v = [], [], []
    for l in range(DEPTH):
        i = l // 2
        mods = [modulation(cm, w_mod[l], b_mod[l]) for cm in cmods]
        for s_idx in range(2):
            x = xs[s_idx]
            latent = s_idx == 1
            sh1, sc1, g1 = mods[s_idx][0], mods[s_idx][1], mods[s_idx][2]
            h = layer_norm(x) * (1 + sc1) + sh1
            if l % 2 == 0:
                if latent:
                    s0 = state_gla[:, i]
                else:
                    s0 = jnp.zeros((BATCH, 2, GLA_HEADS, GLA_DK_HEAD, GLA_DV_HEAD), F32)
                out, s_fin = even_mixer(h, w_in_even[i], w_a2[i], b_a[i], gla_norm_g[i], w_out_even[i], s0)
                if not latent:
                    new_s.append(s_fin)
            else:
                q, k, v = odd_qkv(h, w_qkv[i], q_norm_g[i], k_norm_g[i])
                if latent:
                    q = apply_rope(q, cos, sin)
                    k_rot = apply_rope(k, cos, sin)
                    keys = jnp.concatenate([cache_k[:, i], k_rot], axis=1)
                    vals = jnp.concatenate([cache_v[:, i], v], axis=1)
                    o = block_attention(q, keys, vals)
                else:
                    o = block_attention(q, k, v)
                    new_k.append(k)
                    new_v.append(v)
                out = o.reshape(x.shape) @ w_o[i]
            xs[s_idx] = layer_norm(DEEPNORM_ALPHA * x + g1 * out, ln_g[l, 0], ln_b[l, 0])
        hs = []
        for s_idx in range(2):
            sh2, sc2 = mods[s_idx][3], mods[s_idx][4]
            hs.append((layer_norm(xs[s_idx]) * (1 + sc2) + sh2).reshape(-1, D_MODEL))
        f = moe(jnp.concatenate(hs, axis=0), w_router[l], b_router[l], wgu_bf[l], b_gate_up[l],
                wd_bf[l], b_down[l])
        fs = [f[:N_CTX].reshape(BATCH, SEQ, D_MODEL), f[N_CTX:].reshape(DEC_BATCH, DEC_SEQ, D_MODEL)]
        for s_idx in range(2):
            g2 = mods[s_idx][5]
            xs[s_idx] = layer_norm(DEEPNORM_ALPHA * xs[s_idx] + g2 * fs[s_idx], ln_g[l, 1], ln_b[l, 1])
    return (xs[0], xs[1], jnp.stack(new_s, axis=1), jnp.stack(new_k, axis=1), jnp.stack(new_v, axis=1))
```

```python
import functools

import jax
import jax.numpy as jnp
from jax import lax
from jax.experimental import pallas as pl
from jax.experimental.pallas import tpu as pltpu

F32 = jnp.float32
BF16 = jnp.bfloat16
I32 = jnp.int32

D_MODEL = 1024
BATCH, SEQ = 16, 256
DEC_BATCH, DEC_SEQ = 4, 2048
DEPTH = 4
GRID_W = 64
FNET_WIDTH, FNET_GROUPS, FNET_GROUP_DIM = 512, 4, 128
GLA_HEADS, GLA_DK, GLA_DV = 4, 256, 512
GLA_DK_HEAD, GLA_DV_HEAD = 64, 128
GLA_GATE_RANK, GLA_TAU, GLA_CHUNK = 16, 16.0, 64
EVEN_SPLITS = (512, 768, 1024, 1536, 2048)
ATT_HEADS, ATT_KV_HEADS, HEAD_DIM = 8, 2, 128
ROPE_THETA = 10000.0
ROPE_PAIRS_PER_AXIS = HEAD_DIM // 4
Q_BLOCK = 128
ODD_SPLITS = (ATT_HEADS * HEAD_DIM, (ATT_HEADS + ATT_KV_HEADS) * HEAD_DIM)
N_EXPERTS, TOP_K, D_FF = 32, 4, 1024
SWIGLU_LIMIT, SWIGLU_ALPHA = 7.0, 1.702
LN_EPS, RMS_EPS = 1e-5, 1e-6
DEEPNORM_ALPHA = (2.0 * DEPTH) ** 0.25

N_CTX = BATCH * SEQ
N_LAT = DEC_BATCH * DEC_SEQ
N_TOK = N_CTX + N_LAT
N_ASSIGN = N_TOK * TOP_K
TM = 256
ROWS_MAX = N_ASSIGN + N_EXPERTS * TM
NT_EXP = ROWS_MAX // TM
VMEM_LIMIT = 48 * 1024 * 1024


def _expert_kernel(te_ref, used_ref, xs_ref, wgu_ref, bgu_ref, wd_ref, bd_ref, y_ref):
    del te_ref
    t = pl.program_id(0)

    @pl.when(t < used_ref[0])
    def _():
        gu = jnp.dot(xs_ref[...], wgu_ref[0], preferred_element_type=F32) + bgu_ref[0]
        gate = jnp.minimum(gu[:, :D_FF], SWIGLU_LIMIT)
        up = jnp.clip(gu[:, D_FF:], -SWIGLU_LIMIT, SWIGLU_LIMIT)
        act = (up + 1.0) * gate * jax.nn.sigmoid(SWIGLU_ALPHA * gate)
        y_ref[...] = jnp.dot(act.astype(BF16), wd_ref[0], preferred_element_type=F32) + bd_ref[0]

    @pl.when(t >= used_ref[0])
    def _():
        y_ref[...] = jnp.zeros_like(y_ref)


def expert_mlp(tile_expert, n_used, xs, wgu, bgu, wd, bd):
    grid_spec = pltpu.PrefetchScalarGridSpec(
        num_scalar_prefetch=2,
        grid=(NT_EXP,),
        in_specs=[
            pl.BlockSpec((TM, D_MODEL), lambda t, te, nu: (t, 0)),
            pl.BlockSpec((1, D_MODEL, 2 * D_FF), lambda t, te, nu: (te[t], 0, 0)),
            pl.BlockSpec((1, 1, 2 * D_FF), lambda t, te, nu: (te[t], 0, 0)),
            pl.BlockSpec((1, D_FF, D_MODEL), lambda t, te, nu: (te[t], 0, 0)),
            pl.BlockSpec((1, 1, D_MODEL), lambda t, te, nu: (te[t], 0, 0)),
        ],
        out_specs=pl.BlockSpec((TM, D_MODEL), lambda t, te, nu: (t, 0)),
    )
    return pl.pallas_call(
        _expert_kernel,
        grid_spec=grid_spec,
        out_shape=jax.ShapeDtypeStruct((ROWS_MAX, D_MODEL), F32),
        compiler_params=pltpu.CompilerParams(dimension_semantics=("arbitrary",),
                                             vmem_limit_bytes=VMEM_LIMIT),
        name="expert_mlp",
    )(tile_expert, n_used, xs, wgu, bgu, wd, bd)


def moe(h, w_router, b_router, wgu_bf, b_gu, wd_bf, b_down):
    logits = (h @ w_router + b_router).astype(F32)
    top_v, top_i = lax.top_k(logits, TOP_K)
    top_w = jax.nn.softmax(top_v, axis=-1)

    flat_e = top_i.reshape(-1).astype(I32)
    order = jnp.argsort(flat_e, stable=True).astype(I32)
    counts = jnp.sum(jax.nn.one_hot(flat_e, N_EXPERTS, dtype=I32), axis=0)
    cnt_pad = ((counts + TM - 1) // TM) * TM
    end_pad = jnp.cumsum(cnt_pad)
    start_pad = end_pad - cnt_pad
    start_dense = jnp.cumsum(counts) - counts
    total = end_pad[-1]
    n_used = (total // TM).astype(I32)
    tile_row = jnp.arange(NT_EXP, dtype=I32) * TM
    tile_expert = jnp.minimum(jnp.sum((tile_row[:, None] >= end_pad[None, :]).astype(I32), axis=1),
                              N_EXPERTS - 1)
    tile_expert = jnp.where(tile_row < total, tile_expert, tile_expert[n_used - 1]).astype(I32)
    rows = jnp.arange(ROWS_MAX, dtype=I32)
    e_row = jnp.repeat(tile_expert, TM)
    j = rows - start_pad[e_row]
    valid = jnp.logical_and(j < counts[e_row], rows < total)
    assign = order[jnp.clip(start_dense[e_row] + j, 0, N_ASSIGN - 1)]
    src_tok = jnp.where(valid, assign // TOP_K, 0)
    xs = h.astype(BF16)[src_tok]

    y = expert_mlp(tile_expert, n_used.reshape(1), xs, wgu_bf, b_gu.reshape(N_EXPERTS, 1, 2 * D_FF),
                   wd_bf, b_down.reshape(N_EXPERTS, 1, D_MODEL))

    inv = jnp.zeros((N_ASSIGN,), I32).at[order].set(jnp.arange(N_ASSIGN, dtype=I32), unique_indices=True)
    row_of = (start_pad[flat_e] + inv - start_dense[flat_e]).reshape(N_TOK, TOP_K)
    out = top_w[:, 0:1] * y[row_of[:, 0]]
    for kk in range(1, TOP_K):
        out = out + top_w[:, kk:kk + 1] * y[row_of[:, kk]]
    return out


def layer_norm(x, g=None, b=None):
    mu = jnp.mean(x, axis=-1, keepdims=True)
    var = jnp.mean(jnp.square(x - mu), axis=-1, keepdims=True)
    y = (x - mu) * lax.rsqrt(var + LN_EPS)
    if g is not None:
        y = y * g + b
    return y


def rms_norm(x, g):
    return x * lax.rsqrt(jnp.mean(jnp.square(x), axis=-1, keepdims=True) + RMS_EPS) * g


def modulation(cmod, w_mod_l, b_mod_l):
    m = jax.nn.silu(cmod) @ w_mod_l + b_mod_l
    return jnp.split(m[:, None, :], 6, axis=-1)


def axial_rope(n_tok):
    n_rows = n_tok // GRID_W
    row = jnp.broadcast_to(jnp.arange(n_rows, dtype=F32)[:, None], (n_rows, GRID_W)).reshape(-1)
    col = jnp.broadcast_to(jnp.arange(GRID_W, dtype=F32)[None, :], (n_rows, GRID_W)).reshape(-1)
    freqs = ROPE_THETA ** (-jnp.arange(ROPE_PAIRS_PER_AXIS, dtype=F32) / ROPE_PAIRS_PER_AXIS)
    ang = jnp.concatenate([row[:, None] * freqs, col[:, None] * freqs], axis=-1)
    return jnp.cos(ang), jnp.sin(ang)


def apply_rope(x, cos, sin):
    b, t, h, d = x.shape
    xp = x.reshape(b, t, h, d // 2, 2)
    x0, x1 = xp[..., 0], xp[..., 1]
    c = cos[None, :, None, :]
    s = sin[None, :, None, :]
    return jnp.stack([x0 * c - x1 * s, x0 * s + x1 * c], axis=-1).reshape(b, t, h, d)


def block_attention(q, k, v):
    b, tq, h, hd = q.shape
    g = h // ATT_KV_HEADS
    nb = tq // Q_BLOCK
    qb = q.reshape(b, nb, Q_BLOCK, ATT_KV_HEADS, g, hd).transpose(1, 0, 2, 3, 4, 5)
    scale = hd ** -0.5

    def one_block(qblk):
        s = jnp.einsum('bqkgd,bskd->bkgqs', qblk, k, preferred_element_type=F32) * scale
        p = jax.nn.softmax(s, axis=-1)
        return jnp.einsum('bkgqs,bskd->bqkgd', p, v)

    o = lax.map(one_block, qb)
    return o.transpose(1, 0, 2, 3, 4, 5).reshape(b, tq, h, hd)


GT = 256


def _dot(a, b, **kw):
    return jnp.dot(a, b, preferred_element_type=F32, **kw)


def _dot_nt(a, b):
    return lax.dot_general(a, b, (((1,), (1,)), ((), ())), preferred_element_type=F32)


def _gla_tile(q, k, v, lg, st_ref, fwd):
    nb = GT // GLA_CHUNK
    row = lax.broadcasted_iota(I32, (GT, GT), 0)
    col = lax.broadcasted_iota(I32, (GT, GT), 1)
    causal = (col <= row) if fwd else (col >= row)
    cum = _dot(jnp.where(causal, 1.0, 0.0), lg, precision=lax.Precision.HIGHEST)

    def edge(c):
        r = GLA_CHUNK * c + (GLA_CHUNK - 1 if fwd else 0)
        return cum[r:r + 1, :]

    def mid(c):
        r = GLA_CHUNK * c + GLA_CHUNK // 2
        return cum[r:r + 1, :]

    total = cum[GT - 1:GT, :] if fwd else cum[0:1, :]
    shape = (GLA_CHUNK, GLA_DK)
    e_own = jnp.concatenate([jnp.broadcast_to(edge(c), shape) for c in range(nb)], axis=0)
    m_own = jnp.concatenate([jnp.broadcast_to(mid(c), shape) for c in range(nb)], axis=0)
    rowi = lax.broadcasted_iota(I32, (GT, 1), 0)
    lane = lax.broadcasted_iota(I32, (1, GLA_DK), 1)

    k_edge = k * jnp.exp(e_own - cum)
    q_mid = q * jnp.exp(cum - m_own)
    k_mid = (k * jnp.exp(m_own - cum)).astype(BF16)
    q_in = (q * jnp.exp(cum)).astype(BF16)
    k_out = k * jnp.exp(total - cum)

    srcs = list(range(nb - 1)) if fwd else list(range(nb - 1, 0, -1))
    q_cross, k_cross = [], []
    for c in srcs:
        later = (rowi >= GLA_CHUNK * (c + 1)) if fwd else (rowi < GLA_CHUNK * c)
        in_c = jnp.logical_and(rowi >= GLA_CHUNK * c, rowi < GLA_CHUNK * (c + 1))
        q_cross.append(jnp.where(later, q * jnp.exp(jnp.minimum(cum - edge(c), 0.0)), 0.0))
        k_cross.append(jnp.where(in_c, k_edge, 0.0).astype(BF16))

    same_blk = jnp.right_shift(row, 6) == jnp.right_shift(col, 6)
    inner_mask = jnp.logical_and(same_blk, causal)
    vb = v.astype(BF16)
    st = st_ref[...]
    outs = []
    for h in range(GLA_HEADS):
        in_h = jnp.logical_and(lane >= GLA_DK_HEAD * h, lane < GLA_DK_HEAD * (h + 1))
        s = jnp.where(inner_mask, _dot_nt(jnp.where(in_h, q_mid, 0.0).astype(BF16), k_mid), 0.0)
        for qc, kc in zip(q_cross, k_cross):
            s = s + _dot_nt(jnp.where(in_h, qc, 0.0).astype(BF16), kc)
        vh = vb[:, h * GLA_DV_HEAD:(h + 1) * GLA_DV_HEAD]
        outs.append(_dot(s.astype(BF16), vh))
    o = jnp.concatenate(outs, axis=1) + _dot(q_in, st.astype(BF16))

    tot_col = jnp.sum(jnp.where(row == col, jnp.broadcast_to(total, (GT, GLA_DK)), 0.0),
                      axis=1, keepdims=True)
    kv = _dot(jnp.transpose(k_out).astype(BF16), vb)
    r_head = jnp.right_shift(lax.broadcasted_iota(I32, (GLA_DK, GLA_DV), 0), 6)
    c_head = jnp.right_shift(lax.broadcasted_iota(I32, (GLA_DK, GLA_DV), 1), 7)
    st_ref[...] = jnp.exp(tot_col) * st + jnp.where(r_head == c_head, kv, 0.0)
    return o


def _gla_kernel(qk_ref, v_ref, lg_ref, s0_ref, o_ref, sfin_ref, st_ref, *, nt):
    d = pl.program_id(1)
    j = pl.program_id(2)

    @pl.when(j == 0)
    def _():
        st_ref[...] = jnp.zeros_like(st_ref)
        for h in range(GLA_HEADS):
            st_ref[h * GLA_DK_HEAD:(h + 1) * GLA_DK_HEAD,
                   h * GLA_DV_HEAD:(h + 1) * GLA_DV_HEAD] = s0_ref[0, 0, h]

    for fwd in (True, False):
        @pl.when(d == (0 if fwd else 1))
        def _():
            q = qk_ref[:, :GLA_DK] * (GLA_DK_HEAD ** -0.5)
            k = qk_ref[:, GLA_DK:]
            o_ref[0] = _gla_tile(q, k, v_ref[...], lg_ref[...], st_ref, fwd)

    @pl.when(j == nt - 1)
    def _():
        for h in range(GLA_HEADS):
            sfin_ref[0, 0, h] = st_ref[h * GLA_DK_HEAD:(h + 1) * GLA_DK_HEAD,
                                       h * GLA_DV_HEAD:(h + 1) * GLA_DV_HEAD]


def gla_scan(qk, v, logg, s0, *, seq, n_batch):
    nt = seq // GT
    rows = n_batch * seq

    def tile(b, d, j):
        return b * nt + jnp.where(d == 0, j, nt - 1 - j)

    state_spec = pl.BlockSpec((1, 1, GLA_HEADS, GLA_DK_HEAD, GLA_DV_HEAD), lambda b, d, j: (b, d, 0, 0, 0))
    return pl.pallas_call(
        functools.partial(_gla_kernel, nt=nt),
        grid=(n_batch, 2, nt),
        in_specs=[
            pl.BlockSpec((GT, 2 * GLA_DK), lambda b, d, j: (tile(b, d, j), 0)),
            pl.BlockSpec((GT, GLA_DV), lambda b, d, j: (tile(b, d, j), 0)),
            pl.BlockSpec((GT, GLA_DK), lambda b, d, j: (tile(b, d, j), d)),
            state_spec,
        ],
        out_specs=[
            pl.BlockSpec((1, GT, GLA_DV), lambda b, d, j: (d, tile(b, d, j), 0)),
            state_spec,
        ],
        out_shape=[
            jax.ShapeDtypeStruct((2, rows, GLA_DV), F32),
            jax.ShapeDtypeStruct((n_batch, 2, GLA_HEADS, GLA_DK_HEAD, GLA_DV_HEAD), F32),
        ],
        scratch_shapes=[pltpu.VMEM((GLA_DK, GLA_DV), F32)],
        compiler_params=pltpu.CompilerParams(dimension_semantics=("arbitrary", "arbitrary", "arbitrary"),
                                             vmem_limit_bytes=VMEM_LIMIT),
        name=f"gla_scan_{seq}",
    )(qk, v, logg, s0)


def even_mixer(h, w_in, w_a2, b_a, gla_norm_g, w_out, s0):
    b, t, _ = h.shape
    proj = h @ w_in
    u_f, q, k, v, g, a_lr = jnp.split(proj, EVEN_SPLITS, axis=-1)
    uf = u_f.reshape(b, t, FNET_GROUPS, FNET_GROUP_DIM)
    f_out = jnp.real(jnp.fft.fft2(uf, axes=(1, 3), norm='ortho')).reshape(b, t, FNET_WIDTH)
    a_lr = a_lr.reshape(b, t, 2, GLA_GATE_RANK)
    gate_logits = jnp.einsum('btzr,zre->btze', a_lr, w_a2) + b_a
    logg = (jax.nn.log_sigmoid(gate_logits) / GLA_TAU).reshape(b * t, 2 * GLA_DK)
    qk = jnp.concatenate([q, k], axis=-1).reshape(b * t, 2 * GLA_DK)
    o2, s_fin = gla_scan(qk, v.reshape(b * t, GLA_DV), logg, s0, seq=t, n_batch=b)
    o = (o2[0] + o2[1]).reshape(b, t, GLA_HEADS, GLA_DV_HEAD)
    o = rms_norm(o, gla_norm_g).reshape(b, t, GLA_DV) * jax.nn.silu(g)
    mix = jnp.concatenate([f_out, o], axis=-1)
    return mix @ w_out, s_fin


ATT_GROUP = ATT_HEADS // ATT_KV_HEADS
QT = 256


def _attn_kernel(q_ref, k_ref, v_ref, o_ref):
    k = k_ref[...]
    v = v_ref[...]
    for g in range(ATT_GROUP):
        sl = slice(g * HEAD_DIM, (g + 1) * HEAD_DIM)
        s = _dot_nt(q_ref[:, sl], k)
        p = jnp.exp(s - s.max(axis=-1, keepdims=True))
        den = p.sum(axis=-1, keepdims=True)
        o_ref[:, sl] = _dot(p.astype(BF16), v) / den


def latent_attention(q, keys, vals):
    b, t, _, _ = q.shape
    s_len = keys.shape[1]
    nt = t // QT
    gw = ATT_GROUP * HEAD_DIM
    q2 = (q * (HEAD_DIM ** -0.5)).astype(BF16).reshape(b * t, ATT_HEADS * HEAD_DIM)
    k2 = keys.astype(BF16).reshape(b * s_len, ATT_KV_HEADS * HEAD_DIM)
    v2 = vals.astype(BF16).reshape(b * s_len, ATT_KV_HEADS * HEAD_DIM)
    kv_spec = pl.BlockSpec((s_len, HEAD_DIM), lambda bi, hi, r: (bi, hi))
    o = pl.pallas_call(
        _attn_kernel,
        grid=(b, ATT_KV_HEADS, nt),
        in_specs=[pl.BlockSpec((QT, gw), lambda bi, hi, r: (bi * nt + r, hi)), kv_spec, kv_spec],
        out_specs=pl.BlockSpec((QT, gw), lambda bi, hi, r: (bi * nt + r, hi)),
        out_shape=jax.ShapeDtypeStruct((b * t, ATT_HEADS * HEAD_DIM), F32),
        compiler_params=pltpu.CompilerParams(dimension_semantics=("arbitrary", "arbitrary", "arbitrary"),
                                             vmem_limit_bytes=VMEM_LIMIT),
        name="latent_attention",
    )(q2, k2, v2)
    return o.reshape(b, t, ATT_HEADS, HEAD_DIM)


def odd_qkv(h, w_qkv, q_norm_g, k_norm_g):
    b, t, _ = h.shape
    q, k, v = jnp.split(h @ w_qkv, ODD_SPLITS, axis=-1)
    q = rms_norm(q.reshape(b, t, ATT_HEADS, HEAD_DIM), q_norm_g)
    k = rms_norm(k.reshape(b, t, ATT_KV_HEADS, HEAD_DIM), k_norm_g)
    v = v.reshape(b, t, ATT_KV_HEADS, HEAD_DIM)
    return q, k, v


def kernel(x_prompt, x_sample, state_gla, cache_k, cache_v, c, c_ctx, w_mod, b_mod, ln_g, ln_b,
           w_in_even, w_a2, b_a, gla_norm_g, w_out_even, w_qkv, q_norm_g, k_norm_g, w_o,
           w_router, b_router, w_gate_up, b_gate_up, w_down, b_down):
    wgu_bf = w_gate_up.astype(BF16)
    wd_bf = w_down.astype(BF16)
    cos, sin = axial_rope(DEC_SEQ)
    xs = [x_prompt, x_sample]
    cmods = [c_ctx[None, :], c]
    new_s, new_k, new_v = [], [], []
    for l in range(DEPTH):
        i = l // 2
        mods = [modulation(cm, w_mod[l], b_mod[l]) for cm in cmods]
        for s_idx in range(2):
            x = xs[s_idx]
            latent = s_idx == 1
            sh1, sc1, g1 = mods[s_idx][0], mods[s_idx][1], mods[s_idx][2]
            h = layer_norm(x) * (1 + sc1) + sh1
            if l % 2 == 0:
                if latent:
                    s0 = state_gla[:, i]
                else:
                    s0 = jnp.zeros((BATCH, 2, GLA_HEADS, GLA_DK_HEAD, GLA_DV_HEAD), F32)
                out, s_fin = even_mixer(h, w_in_even[i], w_a2[i], b_a[i], gla_norm_g[i], w_out_even[i], s0)
                if not latent:
                    new_s.append(s_fin)
            else:
                q, k, v = odd_qkv(h, w_qkv[i], q_norm_g[i], k_norm_g[i])
                if latent:
                    q = apply_rope(q, cos, sin)
                    k_rot = apply_rope(k, cos, sin)
                    keys = jnp.concatenate([cache_k[:, i], k_rot], axis=1)
                    vals = jnp.concatenate([cache_v[:, i], v], axis=1)
                    o = latent_attention(q, keys, vals)
                else:
                    o = block_attention(q, k, v)
                    new_k.append(k)
                    new_v.append(v)
                out = o.reshape(x.shape) @ w_o[i]
            xs[s_idx] = layer_norm(DEEPNORM_ALPHA * x + g1 * out, ln_g[l, 0], ln_b[l, 0])
        hs = []
        for s_idx in range(2):
            sh2, sc2 = mods[s_idx][3], mods[s_idx][4]
            hs.append((layer_norm(xs[s_idx]) * (1 + sc2) + sh2).reshape(-1, D_MODEL))
        f = moe(jnp.concatenate(hs, axis=0), w_router[l], b_router[l], wgu_bf[l], b_gate_up[l],
                wd_bf[l], b_down[l])
        fs = [f[:N_CTX].reshape(BATCH, SEQ, D_MODEL), f[N_CTX:].reshape(DEC_BATCH, DEC_SEQ, D_MODEL)]
        for s_idx in range(2):
            g2 = mods[s_idx][5]
            xs[s_idx] = layer_norm(DEEPNORM_ALPHA * xs[s_idx] + g2 * fs[s_idx], ln_g[l, 1], ln_b[l, 1])
    return (xs[0], xs[1], jnp.stack(new_s, axis=1), jnp.stack(new_k, axis=1), jnp.stack(new_v, axis=1))
```
